```python
import math
import jax
import jax.numpy as jnp
from jax import lax
import numpy as np

D_MODEL = 4096
BATCH = 4
SEQ = 2048
DEPTH = 2
DEC_BATCH = 32
DEC_SEQ = 8
PAST_LEN = 16384
PAGE_SIZE = 128

N_BRANCH = 4
SWA_HQ = 16
SWA_HKV = 4
SWA_HD = 64
SWA_WINDOW = 128
HG_HEADS = 8
HG_DK = 128
HG_DV = 128
HG_CHUNK = 16
SSM_CH = 1024
SSM_GSIZE = 16
SSM_GROUPS = SSM_CH // SSM_GSIZE
SSM_P = 64
DIL_PATTERNS = ((128, 1), (512, 4), (2048, 16))
N_DIL = 3
DIL_SLOTS = 6
DIL_HD = 128
DIL_NKEYS = 128
REL_BUCKETS = 32
REL_MAX_DIST = 2048
REL_HEADS = SWA_HQ + N_DIL * DIL_SLOTS
N_EXPERTS = 16
N_EXPERT_GROUPS = 4
TOP_K = 2
D_FF = 1024
ADA_CHUNKS = 6
Q_BLOCK = 128
EPS = 1e-6
NEG_INF = -1e30

A_W = SWA_HQ * SWA_HD
D_W = DIL_SLOTS * DIL_HD
H_W = HG_HEADS * HG_DV
IN_SIZES = (SWA_HQ * SWA_HD, SWA_HKV * SWA_HD, SWA_HKV * SWA_HD,
            N_DIL * DIL_SLOTS * DIL_HD, N_DIL * DIL_SLOTS * DIL_HD, N_DIL * DIL_SLOTS * DIL_HD,
            HG_HEADS * HG_DK, HG_HEADS * HG_DK, HG_HEADS * HG_DV, HG_HEADS * HG_DV,
            SSM_CH, N_BRANCH * D_MODEL)
IN_WIDTH = sum(IN_SIZES)

kernel_name = 'hybrid_gated_branch_decoder_step'


def _qblock(t):
    return Q_BLOCK if t % Q_BLOCK == 0 else t


def rmsnorm(x, g):
    xf = x.astype(jnp.float32)
    xf = xf * lax.rsqrt(jnp.mean(xf * xf, axis=-1, keepdims=True) + EPS)
    return (xf * g.astype(jnp.float32)).astype(x.dtype)


def rel_bucket(dist):
    max_exact = REL_BUCKETS // 2
    d = jnp.maximum(dist, 0)
    ratio = jnp.maximum(d, 1).astype(jnp.float32) / max_exact
    large = max_exact + (jnp.log(ratio) / math.log(REL_MAX_DIST / max_exact)
                         * (REL_BUCKETS - max_exact)).astype(jnp.int32)
    large = jnp.minimum(large, REL_BUCKETS - 1)
    return jnp.where(d < max_exact, d, large)


def swa_attention(q, k_ext, v_ext, prefix_valid, sinks, bias_tab):
    B, T, HQ, hd = q.shape
    HKV = k_ext.shape[2]
    rep = HQ // HKV
    P = k_ext.shape[1] - T
    qb = _qblock(T)
    nb = T // qb
    nk = P + qb
    rows = jnp.arange(qb)[:, None]
    cols = jnp.arange(nk)[None, :]
    dist = P + rows - cols
    in_win = (dist >= 0) & (dist < SWA_WINDOW)
    bias = bias_tab[rel_bucket(dist)].astype(jnp.float32)
    bias = jnp.transpose(bias, (2, 0, 1)).reshape(HKV, rep, qb, nk)
    sink = jnp.broadcast_to(sinks.astype(jnp.float32).reshape(1, HKV, rep, 1, 1), (B, HKV, rep, qb, 1))
    qg = q.reshape(B, nb, qb, HKV, rep, hd)
    scale = hd ** -0.5

    def block(i):
        qi = qg[:, i]
        ki = lax.dynamic_slice_in_dim(k_ext, i * qb, nk, axis=1)
        vi = lax.dynamic_slice_in_dim(v_ext, i * qb, nk, axis=1)
        s = jnp.einsum('bqgrd,bkgd->bgrqk', qi, ki).astype(jnp.float32) * scale + bias
        valid = in_win if prefix_valid else in_win & (i * qb + cols >= P)
        s = jnp.where(valid, s, NEG_INF)
        pr = jax.nn.softmax(jnp.concatenate([s, sink], axis=-1), axis=-1)[..., :-1]
        o = jnp.einsum('bgrqk,bkgd->bqgrd', pr.astype(vi.dtype), vi)
        return o.reshape(B, qb, HQ * hd)

    o = lax.map(block, jnp.arange(nb))
    return jnp.transpose(o, (1, 0, 2, 3)).reshape(B, T, HQ * hd)


def dilated_attention(q, k_exts, v_exts, bias_tab):
    B, T, _, hd = q.shape
    S = DIL_SLOTS
    qb = _qblock(T)
    nb = T // qb
    qg = q.reshape(B, nb, qb, N_DIL, S, hd)
    j = jnp.arange(DIL_NKEYS + 1)
    scale = hd ** -0.5
    biases = []
    for g, (w, d) in enumerate(DIL_PATTERNS):
        bt = bias_tab[:, g * S:(g + 1) * S]
        biases.append(bt[rel_bucket(j * d)].T.astype(jnp.float32))

    def block(i):
        qi = qg[:, i]
        outs, lses = [], []
        for g, (w, d) in enumerate(DIL_PATTERNS):
            P = k_exts[g].shape[1] - T
            qpos = P + i * qb + jnp.arange(qb)
            idx = qpos[:, None] - d * j[None, :]
            valid = idx >= 0
            idx = jnp.maximum(idx, 0)
            kg = jnp.take(k_exts[g], idx, axis=1)
            vg = jnp.take(v_exts[g], idx, axis=1)
            s = jnp.einsum('bqsd,bqjsd->bsqj', qi[:, :, g], kg).astype(jnp.float32) * scale
            s = jnp.where(valid, s + biases[g][None, :, None, :], NEG_INF)
            m = jnp.max(s, axis=-1, keepdims=True)
            e = jnp.exp(s - m)
            den = jnp.sum(e, axis=-1, keepdims=True)
            o = jnp.einsum('bsqj,bqjsd->bqsd', (e / den).astype(vg.dtype), vg)
            outs.append(o)
            lses.append((m + jnp.log(den))[..., 0])
        wts = jax.nn.softmax(jnp.stack(lses, axis=0), axis=0)
        o = outs[0] * jnp.transpose(wts[0], (0, 2, 1))[..., None].astype(outs[0].dtype)
        for g in range(1, N_DIL):
            o = o + outs[g] * jnp.transpose(wts[g], (0, 2, 1))[..., None].astype(outs[g].dtype)
        return o.reshape(B, qb, S * hd)

    o = lax.map(block, jnp.arange(nb))
    return jnp.transpose(o, (1, 0, 2, 3)).reshape(B, T, S * hd)


def hgrn2(q, f_logit, i_in, lb, s0):
    B, T, H, dk = q.shape
    L = math.gcd(T, HG_CHUNK)
    nc = T // L
    q = jax.nn.silu(q)
    log_f = jnp.logaddexp(jnp.log(lb), jnp.log1p(-lb) + jax.nn.log_sigmoid(f_logit))
    k = (1.0 - lb) * jax.nn.sigmoid(-f_logit)

    def chunks(a):
        return a.reshape(B, nc, L, H, a.shape[-1]).transpose(1, 0, 3, 2, 4)

    causal = (jnp.arange(L)[:, None] >= jnp.arange(L)[None, :])[None, None, :, :, None]

    def step(S, inp):
        qc, kc, vc, lfc = inp
        b = jnp.cumsum(lfc, axis=2)
        o = jnp.einsum('bhtc,bhcv->bhtv', qc * jnp.exp(b), S)
        decay = jnp.exp(jnp.where(causal, b[:, :, :, None, :] - b[:, :, None, :, :], -jnp.inf))
        a = jnp.einsum('bhtc,bhsc,bhtsc->bhts', qc, kc, decay)
        o = o + jnp.einsum('bhts,bhsv->bhtv', a, vc)
        b_end = b[:, :, -1:, :]
        S = jnp.exp(b_end[:, :, 0, :, None]) * S + jnp.einsum('bhsc,bhsv->bhcv', kc * jnp.exp(b_end - b), vc)
        return S, o

    S, o = lax.scan(step, s0, (chunks(q), chunks(k), chunks(i_in), chunks(log_f)))
    return o.transpose(1, 0, 3, 2, 4).reshape(B, T, H, -1), S


def s5(u, a_re, a_im, b_re, b_im, c_re, c_im, d_skip, log_dt, w_glu, h0):
    B, T, _ = u.shape
    ug = u.reshape(B, T, SSM_GROUPS, SSM_GSIZE)
    dt = jnp.exp(log_dt)[:, None]
    ar, ai = dt * a_re, dt * a_im
    mag = jnp.exp(ar)
    abar_re, abar_im = mag * jnp.cos(ai), mag * jnp.sin(ai)
    den = a_re * a_re + a_im * a_im
    nr, ni = abar_re - 1.0, abar_im
    coef_re = (nr * a_re + ni * a_im) / den
    coef_im = (ni * a_re - nr * a_im) / den
    bbar_re = coef_re[..., None] * b_re - coef_im[..., None] * b_im
    bbar_im = coef_re[..., None] * b_im + coef_im[..., None] * b_re
    bu_re = jnp.einsum('gpi,btgi->btgp', bbar_re, ug)
    bu_im = jnp.einsum('gpi,btgi->btgp', bbar_im, ug)
    shape = bu_re.shape

    def combine(e1, e2):
        a1r, a1i, b1r, b1i = e1
        a2r, a2i, b2r, b2i = e2
        return (a2r * a1r - a2i * a1i, a2r * a1i + a2i * a1r,
                a2r * b1r - a2i * b1i + b2r, a2r * b1i + a2i * b1r + b2i)

    _, _, hr, hi = lax.associative_scan(
        combine, (jnp.broadcast_to(abar_re, shape), jnp.broadcast_to(abar_im, shape), bu_re, bu_im), axis=1)
    if h0 is not None:
        n = jnp.arange(1, T + 1, dtype=jnp.float32)[:, None, None]
        pm = jnp.exp(n * ar)
        pr, pi = pm * jnp.cos(n * ai), pm * jnp.sin(n * ai)
        h0r, h0i = h0[:, 0][:, None], h0[:, 1][:, None]
        hr, hi = hr + pr * h0r - pi * h0i, hi + pr * h0i + pi * h0r
    y = jnp.einsum('gip,btgp->btgi', c_re, hr) - jnp.einsum('gip,btgp->btgi', c_im, hi)
    y = y.reshape(B, T, SSM_CH) + d_skip * u
    y = jax.nn.gelu(y)
    y = y * jax.nn.sigmoid(y @ w_glu)
    return y, jnp.stack([hr[:, -1], hi[:, -1]], axis=1)


def moe(xn, router_w, router_b, w1, w3, w2):
    f32 = jnp.float32
    B, T, _ = xn.shape
    s = jax.nn.sigmoid((xn @ router_w).astype(f32))
    sb = s + router_b.astype(f32)
    epg = N_EXPERTS // N_EXPERT_GROUPS
    gscore = lax.top_k(sb.reshape(B, T, N_EXPERT_GROUPS, epg), TOP_K)[0].sum(-1)
    gsel = jnp.argmax(gscore, axis=-1)
    in_grp = (jnp.arange(N_EXPERTS) // epg)[None, None, :] == gsel[..., None]
    _, idx = lax.top_k(jnp.where(in_grp, sb, -jnp.inf), TOP_K)
    w = jnp.take_along_axis(s, idx, axis=-1)
    w = w / jnp.sum(w, axis=-1, keepdims=True)
    comb = jnp.sum(jax.nn.one_hot(idx, N_EXPERTS, dtype=f32) * w[..., None], axis=-2)
    h = jax.nn.silu(jnp.einsum('btd,edf->btef', xn, w1)) * jnp.einsum('btd,edf->btef', xn, w3)
    h = h * comb[..., None].astype(h.dtype)
    return jnp.einsum('btef,efd->btd', h, w2)


def mixer_block(xn, l, past, p):
    f32 = jnp.float32
    B, T, _ = xn.shape
    dt = xn.dtype
    (qa, ka, va, qd, kd, vd, qh, fh, ih, gh, us, gl) = jnp.split(
        xn @ p['w_in'][l], np.cumsum(IN_SIZES)[:-1].tolist(), axis=-1)
    qa = qa.reshape(B, T, SWA_HQ, SWA_HD)
    ka = ka.reshape(B, T, SWA_HKV, SWA_HD)
    va = va.reshape(B, T, SWA_HKV, SWA_HD)
    if past is None:
        pad = jnp.zeros((B, SWA_WINDOW, SWA_HKV, SWA_HD), dt)
        ka_ext = jnp.concatenate([pad, ka], axis=1)
        va_ext = jnp.concatenate([pad, va], axis=1)
        n_keep = min(SWA_WINDOW, T)
    else:
        buf = past[0]
        ka_ext = jnp.concatenate([buf[:, 0], ka], axis=1)
        va_ext = jnp.concatenate([buf[:, 1], va], axis=1)
        n_keep = min(SWA_WINDOW, buf.shape[2] + T)
    oa = swa_attention(qa, ka_ext, va_ext, past is not None, p['swa_sinks'][l], p['rel_bias'][:, :SWA_HQ])
    new_swa = jnp.stack([ka_ext[:, -n_keep:], va_ext[:, -n_keep:]], axis=1)
    qd = qd.reshape(B, T, N_DIL * DIL_SLOTS, DIL_HD)
    kd = kd.reshape(B, T, N_DIL * DIL_SLOTS, DIL_HD)
    vd = vd.reshape(B, T, N_DIL * DIL_SLOTS, DIL_HD)
    k_exts, v_exts, new_dil = [], [], []
    for g, (w, _) in enumerate(DIL_PATTERNS):
        kg = kd[:, :, g * DIL_SLOTS:(g + 1) * DIL_SLOTS]
        vg = vd[:, :, g * DIL_SLOTS:(g + 1) * DIL_SLOTS]
        if past is not None:
            buf = past[1][g]
            kg = jnp.concatenate([buf[:, 0], kg], axis=1)
            vg = jnp.concatenate([buf[:, 1], vg], axis=1)
        n_keep = min(w, kg.shape[1])
        new_dil.append(jnp.stack([kg[:, -n_keep:], vg[:, -n_keep:]], axis=1))
        k_exts.append(kg)
        v_exts.append(vg)
    od = dilated_attention(qd, k_exts, v_exts, p['rel_bias'][:, SWA_HQ:])
    lb_all = jnp.cumsum(jax.nn.softmax(p['hgrn_lb'].astype(f32), axis=0), axis=0)
    lb = (lb_all[l] - lb_all[0]).reshape(HG_HEADS, HG_DK)
    s0 = jnp.zeros((B, HG_HEADS, HG_DK, HG_DV), f32) if past is None else past[2].astype(f32)
    oh, new_h = hgrn2(qh.reshape(B, T, HG_HEADS, HG_DK).astype(f32), fh.reshape(B, T, HG_HEADS, HG_DK).astype(f32),
                      ih.reshape(B, T, HG_HEADS, HG_DV).astype(f32), lb, s0)
    oh = oh * lax.rsqrt(jnp.mean(oh * oh, axis=-1, keepdims=True) + EPS)
    oh = (oh.reshape(B, T, H_W) * p['hgrn_norm'][l].astype(f32) * jax.nn.silu(gh.astype(f32))).astype(dt)
    h0 = None if past is None else past[3].astype(f32)
    osm, new_s = s5(us.astype(f32), p['ssm_a_re'][l].astype(f32), p['ssm_a_im'][l].astype(f32),
                    p['ssm_b_re'][l].astype(f32), p['ssm_b_im'][l].astype(f32),
                    p['ssm_c_re'][l].astype(f32), p['ssm_c_im'][l].astype(f32),
                    p['ssm_d'][l].astype(f32), p['ssm_log_dt'][l].astype(f32), p['ssm_w_glu'][l].astype(f32), h0)
    gates = jax.nn.sigmoid(gl.astype(f32)).reshape(B, T, N_BRANCH, D_MODEL).astype(dt)
    merged = (gates[:, :, 0] * (oa @ p['w_br_a'][l]) + gates[:, :, 1] * (oh @ p['w_br_h'][l])
              + gates[:, :, 2] * (osm.astype(dt) @ p['w_br_s'][l]) + gates[:, :, 3] * (od @ p['w_br_d'][l]))
    out = merged @ p['w_out'][l]
    return out, (new_swa, new_dil[0], new_dil[1], new_dil[2], new_h.astype(dt), new_s.astype(dt))


def trunk(x, c, past, p):
    news = []
    for l in range(DEPTH):
        mod = jax.nn.silu(c) @ p['w_ada'][l] + p['b_ada'][l]
        sh1, sc1, g1, sh2, sc2, g2 = jnp.split(mod[:, None, :], ADA_CHUNKS, axis=-1)
        past_l = None if past is None else (past[0][l], tuple(cd[l] for cd in past[1]), past[2][l], past[3][l])
        xn = rmsnorm(x, p['norm_mix'][l]) * (1 + sc1) + sh1
        m, new = mixer_block(xn, l, past_l, p)
        x = x + g1 * m
        xn = rmsnorm(x, p['norm_ffn'][l]) * (1 + sc2) + sh2
        x = x + g2 * moe(xn, p['router_w'], p['router_b'], p['moe_w1'][l], p['moe_w3'][l], p['moe_w2'][l])
        news.append(new)
    y = rmsnorm(x, p['final_norm'])
    stacked = [jnp.stack([nw[i] for nw in news], axis=0) for i in range(6)]
    return y, stacked


def setup_inputs(seed: int = 0) -> dict:
    key = jax.random.key(seed)
    ks = iter(jax.random.split(key, 48))
    f32 = jnp.float32

    def nrm(shape, scale=1.0):
        return jax.random.normal(next(ks), shape, f32) * scale

    swa_rows = min(SWA_WINDOW, PAST_LEN)
    dil_rows = [min(w, PAST_LEN) for w, _ in DIL_PATTERNS]
    return {
        'x_prompt': nrm((BATCH, SEQ, D_MODEL)),
        'x_sample': nrm((DEC_BATCH, DEC_SEQ, D_MODEL)),
        'c_prompt': nrm((BATCH, D_MODEL)),
        'c_sample': nrm((DEC_BATCH, D_MODEL)),
        'cache_swa': nrm((DEPTH, DEC_BATCH, 2, swa_rows, SWA_HKV, SWA_HD)),
        'cache_dil1': nrm((DEPTH, DEC_BATCH, 2, dil_rows[0], DIL_SLOTS, DIL_HD)),
        'cache_dil2': nrm((DEPTH, DEC_BATCH, 2, dil_rows[1], DIL_SLOTS, DIL_HD)),
        'cache_dil3': nrm((DEPTH, DEC_BATCH, 2, dil_rows[2], DIL_SLOTS, DIL_HD)),
        'state_hgrn': nrm((DEPTH, DEC_BATCH, HG_HEADS, HG_DK, HG_DV), 0.5),
        'state_ssm': nrm((DEPTH, DEC_BATCH, 2, SSM_GROUPS, SSM_P), 0.5),
        'w_ada': nrm((DEPTH, D_MODEL, ADA_CHUNKS * D_MODEL), 0.5 * D_MODEL ** -0.5),
        'b_ada': nrm((DEPTH, ADA_CHUNKS * D_MODEL), 0.02),
        'norm_mix': 1.0 + nrm((DEPTH, D_MODEL), 0.02),
        'norm_ffn': 1.0 + nrm((DEPTH, D_MODEL), 0.02),
        'w_in': nrm((DEPTH, D_MODEL, IN_WIDTH), D_MODEL ** -0.5),
        'swa_sinks': nrm((DEPTH, SWA_HQ)),
        'rel_bias': nrm((REL_BUCKETS, REL_HEADS), 0.5),
        'hgrn_lb': nrm((DEPTH, HG_HEADS * HG_DK)),
        'hgrn_norm': 1.0 + nrm((DEPTH, H_W), 0.02),
        'ssm_a_re': -0.5 + nrm((DEPTH, SSM_GROUPS, SSM_P), 0.01),
        'ssm_a_im': jnp.pi * jnp.arange(SSM_P, dtype=f32) + nrm((DEPTH, SSM_GROUPS, SSM_P), 0.01),
        'ssm_b_re': nrm((DEPTH, SSM_GROUPS, SSM_P, SSM_GSIZE), (2 * SSM_GSIZE) ** -0.5),
        'ssm_b_im': nrm((DEPTH, SSM_GROUPS, SSM_P, SSM_GSIZE), (2 * SSM_GSIZE) ** -0.5),
        'ssm_c_re': nrm((DEPTH, SSM_GROUPS, SSM_GSIZE, SSM_P), (2 * SSM_P) ** -0.5),
        'ssm_c_im': nrm((DEPTH, SSM_GROUPS, SSM_GSIZE, SSM_P), (2 * SSM_P) ** -0.5),
        'ssm_d': nrm((DEPTH, SSM_CH)),
        'ssm_log_dt': jax.random.uniform(next(ks), (DEPTH, SSM_GROUPS), f32, math.log(1e-3), math.log(1e-1)),
        'ssm_w_glu': nrm((DEPTH, SSM_CH, SSM_CH), SSM_CH ** -0.5),
        'w_br_a': nrm((DEPTH, A_W, D_MODEL), A_W ** -0.5),
        'w_br_h': nrm((DEPTH, H_W, D_MODEL), H_W ** -0.5),
        'w_br_s': nrm((DEPTH, SSM_CH, D_MODEL), SSM_CH ** -0.5),
        'w_br_d': nrm((DEPTH, D_W, D_MODEL), D_W ** -0.5),
        'w_out': nrm((DEPTH, D_MODEL, D_MODEL), D_MODEL ** -0.5),
        'router_w': nrm((D_MODEL, N_EXPERTS), D_MODEL ** -0.5),
        'router_b': nrm((N_EXPERTS,), 0.01),
        'moe_w1': nrm((DEPTH, N_EXPERTS, D_MODEL, D_FF), D_MODEL ** -0.5),
        'moe_w3': nrm((DEPTH, N_EXPERTS, D_MODEL, D_FF), D_MODEL ** -0.5),
        'moe_w2': nrm((DEPTH, N_EXPERTS, D_FF, D_MODEL), D_FF ** -0.5),
        'final_norm': 1.0 + nrm((D_MODEL,), 0.02),
    }


def reference(x_prompt, x_sample, c_prompt, c_sample, cache_swa, cache_dil1, cache_dil2, cache_dil3,
              state_hgrn, state_ssm, w_ada, b_ada, norm_mix, norm_ffn, w_in, swa_sinks, rel_bias,
              hgrn_lb, hgrn_norm, ssm_a_re, ssm_a_im, ssm_b_re, ssm_b_im, ssm_c_re, ssm_c_im, ssm_d,
              ssm_log_dt, ssm_w_glu, w_br_a, w_br_h, w_br_s, w_br_d, w_out, router_w, router_b,
              moe_w1, moe_w3, moe_w2, final_norm):
    p = {'w_ada': w_ada, 'b_ada': b_ada, 'norm_mix': norm_mix, 'norm_ffn': norm_ffn, 'w_in': w_in,
         'swa_sinks': swa_sinks, 'rel_bias': rel_bias, 'hgrn_lb': hgrn_lb, 'hgrn_norm': hgrn_norm,
         'ssm_a_re': ssm_a_re, 'ssm_a_im': ssm_a_im, 'ssm_b_re': ssm_b_re, 'ssm_b_im': ssm_b_im,
         'ssm_c_re': ssm_c_re, 'ssm_c_im': ssm_c_im, 'ssm_d': ssm_d, 'ssm_log_dt': ssm_log_dt,
         'ssm_w_glu': ssm_w_glu, 'w_br_a': w_br_a, 'w_br_h': w_br_h, 'w_br_s': w_br_s, 'w_br_d': w_br_d,
         'w_out': w_out, 'router_w': router_w, 'router_b': router_b, 'moe_w1': moe_w1, 'moe_w3': moe_w3,
         'moe_w2': moe_w2, 'final_norm': final_norm}
    y_prompt, st_p = trunk(x_prompt, c_prompt, None, p)
    past = (cache_swa, (cache_dil1, cache_dil2, cache_dil3), state_hgrn, state_ssm)
    y_sample, st_s = trunk(x_sample, c_sample, past, p)
    return (y_prompt, y_sample, st_p[0], st_p[1], st_p[2], st_p[3], st_p[4], st_p[5],
            st_s[0], st_s[1], st_s[2], st_s[3], st_s[4], st_s[5])
```

```python
import functools
import math

import jax
import jax.numpy as jnp
import numpy as np
from jax import lax
from jax.experimental import pallas as pl
from jax.experimental.pallas import tpu as pltpu

F32 = jnp.float32
BF16 = jnp.bfloat16

SWA_HQ, SWA_HKV, SWA_HD, SWA_WINDOW = 16, 4, 64, 128
HG_HEADS, HG_DK, HG_DV, HG_CHUNK = 8, 128, 128, 16
SSM_CH, SSM_GSIZE, SSM_P = 1024, 16, 64
SSM_GROUPS = SSM_CH // SSM_GSIZE
DIL_PATTERNS = ((128, 1), (512, 4), (2048, 16))
N_DIL, DIL_SLOTS, DIL_HD, DIL_NKEYS = 3, 6, 128, 128
REL_BUCKETS, REL_MAX_DIST = 32, 2048
N_EXPERTS, N_EXPERT_GROUPS, TOP_K = 16, 4, 2
ADA_CHUNKS = 6
Q_BLOCK = 128
EPS = 1e-6
NEG_INF = -1e30

A_W = SWA_HQ * SWA_HD
KV_W = SWA_HKV * SWA_HD
D_W = DIL_SLOTS * DIL_HD
H_W = HG_HEADS * HG_DV
SEG_A = A_W + 2 * KV_W
SEG_D = 3 * N_DIL * D_W
SEG_H = 4 * H_W
SEG_S = SSM_CH
OFF_A, OFF_D = 0, SEG_A
OFF_H = OFF_D + SEG_D
OFF_S = OFF_H + SEG_H
OFF_G = OFF_S + SEG_S

SSM_LANES = SSM_GROUPS * SSM_P
SSM_BLK_IN = 128
SSM_BLK_ST = 512
SSM_NBLK = SSM_CH // SSM_BLK_IN

VMEM_LIMIT_BYTES_V7X = 56 * 1024 * 1024


def _cp(*sem):
    return pltpu.CompilerParams(dimension_semantics=sem, vmem_limit_bytes=VMEM_LIMIT_BYTES_V7X)


def _tile(n, pref, mult=8):
    if n <= pref:
        return n
    t = (pref // mult) * mult
    while t > mult and n % t:
        t -= mult
    assert n % t == 0, (n, pref)
    return t


def _dot(a, b):
    return jnp.dot(a, b, preferred_element_type=F32)


def _dot_nt(a, b):
    return lax.dot_general(a, b, (((1,), (1,)), ((), ())), preferred_element_type=F32)


def _dot_tn(a, b):
    return lax.dot_general(a, b, (((0,), (0,)), ((), ())), preferred_element_type=F32)


def _ada_kernel(c_ref, w_ref, b_ref, o_ref):
    c = c_ref[...]
    a = (c * jax.nn.sigmoid(c)).astype(BF16)
    o_ref[...] = _dot(a, w_ref[...].astype(BF16)) + b_ref[...]


def ada_modulation(c_all, w_ada, b_ada):
    depth, d, n = w_ada.shape
    bp = c_all.shape[0]
    tn = _tile(n, 512, 128)
    return pl.pallas_call(
        _ada_kernel,
        grid=(depth, n // tn),
        in_specs=[pl.BlockSpec((bp, d), lambda l, j: (0, 0)),
                  pl.BlockSpec((None, d, tn), lambda l, j: (l, 0, j)),
                  pl.BlockSpec((None, 1, tn), lambda l, j: (l, 0, j))],
        out_specs=pl.BlockSpec((None, bp, tn), lambda l, j: (l, 0, j)),
        out_shape=jax.ShapeDtypeStruct((depth, bp, n), F32),
        compiler_params=_cp("arbitrary", "arbitrary"),
    )(c_all, w_ada, b_ada.reshape(depth, 1, n))


def _norm_kernel(*refs, has_resid, has_mod, emit_x):
    it = iter(refs)
    x_ref = next(it)
    if has_resid:
        y_ref, gate_ref = next(it), next(it)
    g_ref = next(it)
    if has_mod:
        sc_ref, sh_ref = next(it), next(it)
    if emit_x:
        xo_ref = next(it)
    o_ref = next(it)
    x = x_ref[...]
    if has_resid:
        x = x + gate_ref[...] * y_ref[...]
    if emit_x:
        xo_ref[...] = x
    xn = x * lax.rsqrt(jnp.mean(x * x, axis=-1, keepdims=True) + EPS) * g_ref[...]
    if has_mod:
        xn = xn * (1.0 + sc_ref[...]) + sh_ref[...]
    o_ref[...] = xn.astype(o_ref.dtype)


def norm_stage(x3, g, mod=None, mod_chunks=None, resid=None, gate_mod=None, gate_chunk=None, out_dtype=BF16):
    if gate_mod is None:
        gate_mod = mod
    G, R, D = x3.shape
    tm = _tile(R, 256)
    has_resid, has_mod = resid is not None, mod_chunks is not None
    rm = None if mod is None else mod.shape[1]

    def mod_spec(chunk):
        if rm == 1:
            return pl.BlockSpec((None, 1, D), lambda b, i: (b, 0, chunk))
        return pl.BlockSpec((None, tm, D), lambda b, i: (b, i, chunk))

    xspec = pl.BlockSpec((None, tm, D), lambda b, i: (b, i, 0))
    args, specs = [x3], [xspec]
    if has_resid:
        args += [resid, gate_mod]
        specs += [xspec, mod_spec(gate_chunk)]
    args.append(g.reshape(1, D))
    specs.append(pl.BlockSpec((1, D), lambda b, i: (0, 0)))
    if has_mod:
        args += [mod, mod]
        specs += [mod_spec(mod_chunks[0]), mod_spec(mod_chunks[1])]
    out_shape, out_specs = [], []
    if has_resid:
        out_shape.append(jax.ShapeDtypeStruct((G, R, D), F32))
        out_specs.append(xspec)
    out_shape.append(jax.ShapeDtypeStruct((G, R, D), out_dtype))
    out_specs.append(xspec)
    outs = pl.pallas_call(
        functools.partial(_norm_kernel, has_resid=has_resid, has_mod=has_mod, emit_x=has_resid),
        grid=(G, R // tm), in_specs=specs, out_specs=out_specs, out_shape=out_shape,
        compiler_params=_cp("arbitrary", "arbitrary"),
    )(*args)
    return (outs[0], outs[1]) if has_resid else (None, outs[0])


def _mm_kernel(x_ref, w_ref, o_ref, *, act):
    acc = _dot(x_ref[...], w_ref[...].astype(BF16))
    if act == "sigmoid":
        acc = jax.nn.sigmoid(acc)
    o_ref[...] = acc.astype(o_ref.dtype)


def matmul_seg(x3, w, layer, col_off, width, tn, out_dtype, act=None, tm_pref=2048):
    G, R, K = x3.shape
    tm = _tile(R, tm_pref)
    assert width % tn == 0 and col_off % tn == 0
    ob = col_off // tn
    return pl.pallas_call(
        functools.partial(_mm_kernel, act=act),
        grid=(G, R // tm, width // tn),
        in_specs=[pl.BlockSpec((None, tm, K), lambda b, i, j: (b, i, 0)),
                  pl.BlockSpec((None, K, tn), lambda b, i, j: (layer, 0, ob + j))],
        out_specs=pl.BlockSpec((None, tm, tn), lambda b, i, j: (b, i, j)),
        out_shape=jax.ShapeDtypeStruct((G, R, width), out_dtype),
        compiler_params=_cp("arbitrary", "arbitrary", "arbitrary"),
    )(x3, w)


def _merge_kernel(oa_ref, oh_ref, os_ref, od_ref, wa_ref, wh_ref, ws_ref, wd_ref,
                  ga_ref, gh_ref, gs_ref, gd_ref, o_ref):
    acc = ga_ref[...].astype(F32) * _dot(oa_ref[...], wa_ref[...].astype(BF16))
    acc = acc + gh_ref[...].astype(F32) * _dot(oh_ref[...], wh_ref[...].astype(BF16))
    acc = acc + gs_ref[...].astype(F32) * _dot(os_ref[...], ws_ref[...].astype(BF16))
    acc = acc + gd_ref[...].astype(F32) * _dot(od_ref[...], wd_ref[...].astype(BF16))
    o_ref[...] = acc.astype(o_ref.dtype)


def merge_branches(oa, oh, osm, od, gates, w_a, w_h, w_s, w_d, layer):
    G, R, _ = oa.shape
    D = w_a.shape[-1]
    tm, tn = _tile(R, 1024), _tile(D, 256, 128)
    nb = D // tn

    def ospec(width):
        return pl.BlockSpec((None, tm, width), lambda b, i, j: (b, i, 0))

    def wspec(k):
        return pl.BlockSpec((None, k, tn), lambda b, i, j: (layer, 0, j))

    def gspec(br):
        return pl.BlockSpec((None, tm, tn), lambda b, i, j: (b, i, br * nb + j))

    return pl.pallas_call(
        _merge_kernel,
        grid=(G, R // tm, nb),
        in_specs=[ospec(A_W), ospec(H_W), ospec(SSM_CH), ospec(D_W),
                  wspec(A_W), wspec(H_W), wspec(SSM_CH), wspec(D_W),
                  gspec(0), gspec(1), gspec(2), gspec(3)],
        out_specs=pl.BlockSpec((None, tm, tn), lambda b, i, j: (b, i, j)),
        out_shape=jax.ShapeDtypeStruct((G, R, D), BF16),
        compiler_params=_cp("arbitrary", "arbitrary", "arbitrary"),
    )(oa, oh, osm, od, w_a, w_h, w_s, w_d, gates, gates, gates, gates)


def rel_bucket(dist):
    max_exact = REL_BUCKETS // 2
    d = jnp.maximum(dist, 0)
    ratio = jnp.maximum(d, 1).astype(F32) / max_exact
    large = max_exact + (jnp.log(ratio) / math.log(REL_MAX_DIST / max_exact)
                         * (REL_BUCKETS - max_exact)).astype(jnp.int32)
    large = jnp.minimum(large, REL_BUCKETS - 1)
    return jnp.where(d < max_exact, d, large)


def _attn_kernel(*refs, hk, rep, hd, n_prev, scale, mask_first_prev, has_sink, want_lse, blk_axis):
    it = iter(refs)
    q_ref, kp_ref, vp_ref, kc_ref, vc_ref, bias_ref = (next(it) for _ in range(6))
    sink_ref = next(it) if has_sink else None
    o_ref = next(it)
    lse_ref = next(it) if want_lse else None
    prev_add = 0.0
    if mask_first_prev:
        prev_add = jnp.where(pl.program_id(blk_axis) > 0, 0.0, NEG_INF).astype(F32)
    for g in range(hk):
        cs = slice(g * hd, (g + 1) * hd)
        kp, vp = kp_ref[:, cs].astype(BF16), vp_ref[:, cs].astype(BF16)
        kc, vc = kc_ref[:, cs].astype(BF16), vc_ref[:, cs].astype(BF16)
        for r in range(rep):
            h = g * rep + r
            hs = slice(h * hd, (h + 1) * hd)
            q = q_ref[:, hs].astype(BF16)
            sp = _dot_nt(q, kp) * scale + bias_ref[h, :, :n_prev] + prev_add
            sc = _dot_nt(q, kc) * scale + bias_ref[h, :, n_prev:]
            m = jnp.maximum(jnp.max(sp, axis=-1, keepdims=True), jnp.max(sc, axis=-1, keepdims=True))
            if has_sink:
                m = jnp.maximum(m, sink_ref[h])
            ep, ec = jnp.exp(sp - m), jnp.exp(sc - m)
            den = jnp.sum(ep, axis=-1, keepdims=True) + jnp.sum(ec, axis=-1, keepdims=True)
            if has_sink:
                den = den + jnp.exp(sink_ref[h] - m)
            o = _dot(ep.astype(BF16), vp) + _dot(ec.astype(BF16), vc)
            o_ref[:, hs] = (o / den).astype(o_ref.dtype)
            if want_lse:
                lse_ref[:, hs] = jnp.broadcast_to(m + jnp.log(den), o.shape)


def banded_attention(arrays, specs, bias, sinks, grid, out_shape, out_spec, *, hk, rep, hd, n_prev,
                     mask_first_prev, want_lse, blk_axis, out_dtype):
    nidx = len(grid)
    in_specs = list(specs) + [pl.BlockSpec(bias.shape, lambda *a: (0, 0, 0))]
    args = list(arrays) + [bias]
    if sinks is not None:
        in_specs.append(pl.BlockSpec(memory_space=pltpu.SMEM))
        args.append(sinks)
    out_shapes = [jax.ShapeDtypeStruct(out_shape, out_dtype)]
    out_specs = [out_spec]
    if want_lse:
        out_shapes.append(jax.ShapeDtypeStruct(out_shape, F32))
        out_specs.append(out_spec)
    kern = functools.partial(_attn_kernel, hk=hk, rep=rep, hd=hd, n_prev=n_prev, scale=hd ** -0.5,
                             mask_first_prev=mask_first_prev, has_sink=sinks is not None,
                             want_lse=want_lse, blk_axis=blk_axis)
    outs = pl.pallas_call(kern, grid=grid, in_specs=in_specs, out_specs=out_specs, out_shape=out_shapes,
                          compiler_params=_cp(*(("arbitrary",) * nidx)))(*args)
    return outs if want_lse else outs[0]


def _band_bias(tab, qb, n_prev, dilation, lo, hi):
    rows = jnp.arange(qb)[:, None]
    cols = jnp.arange(n_prev + qb)[None, :]
    dist = n_prev + rows - cols
    b = tab[rel_bucket(dist * dilation)].astype(F32)
    b = jnp.where(((dist >= lo) & (dist <= hi))[:, :, None], b, NEG_INF)
    return jnp.transpose(b, (2, 0, 1))


def _strided_bias(tab, t_new, n_prev, dilation):
    rows = jnp.arange(t_new)[:, None]
    cols = jnp.arange(n_prev + t_new)[None, :]
    delta = n_prev + rows - cols
    j = delta // dilation
    ok = (delta >= 0) & (delta % dilation == 0) & (j <= DIL_NKEYS)
    b = tab[rel_bucket(j * dilation)].astype(F32)
    b = jnp.where(ok[:, :, None], b, NEG_INF)
    return jnp.transpose(b, (2, 0, 1))


def swa_prompt(proj_a, sinks, tab):
    B, T, _ = proj_a.shape
    qb = Q_BLOCK if T % Q_BLOCK == 0 else T
    nb = T // qb
    bias = _band_bias(tab, qb, SWA_WINDOW, 1, 0, SWA_WINDOW - 1)
    assert qb == SWA_WINDOW
    kb, vb = A_W // KV_W, A_W // KV_W + 1
    specs = [pl.BlockSpec((None, qb, A_W), lambda b, i: (b, i, 0)),
             pl.BlockSpec((None, qb, KV_W), lambda b, i: (b, jnp.maximum(i - 1, 0), kb)),
             pl.BlockSpec((None, qb, KV_W), lambda b, i: (b, jnp.maximum(i - 1, 0), vb)),
             pl.BlockSpec((None, qb, KV_W), lambda b, i: (b, i, kb)),
             pl.BlockSpec((None, qb, KV_W), lambda b, i: (b, i, vb))]
    return banded_attention([proj_a] * 5, specs, bias, sinks, (B, nb), (B, T, A_W),
                            pl.BlockSpec((None, qb, A_W), lambda b, i: (b, i, 0)),
                            hk=SWA_HKV, rep=SWA_HQ // SWA_HKV, hd=SWA_HD, n_prev=qb,
                            mask_first_prev=True, want_lse=False, blk_axis=1, out_dtype=BF16)


def swa_sample(proj_a, cache, sinks, tab):
    B, T, _ = proj_a.shape
    P = cache.shape[2]
    bias = _band_bias(tab, T, P, 1, 0, SWA_WINDOW - 1)
    kb, vb = A_W // KV_W, A_W // KV_W + 1
    specs = [pl.BlockSpec((None, T, A_W), lambda b: (b, 0, 0)),
             pl.BlockSpec((None, None, P, KV_W), lambda b: (b, 0, 0, 0)),
             pl.BlockSpec((None, None, P, KV_W), lambda b: (b, 1, 0, 0)),
             pl.BlockSpec((None, T, KV_W), lambda b: (b, 0, kb)),
             pl.BlockSpec((None, T, KV_W), lambda b: (b, 0, vb))]
    return banded_attention([proj_a, cache, cache, proj_a, proj_a], specs, bias, sinks, (B,), (B, T, A_W),
                            pl.BlockSpec((None, T, A_W), lambda b: (b, 0, 0)),
                            hk=SWA_HKV, rep=SWA_HQ // SWA_HKV, hd=SWA_HD, n_prev=P,
                            mask_first_prev=False, want_lse=False, blk_axis=0, out_dtype=BF16)


def _combine_kernel(o0, o1, o2, l0, l1, l2, out_ref):
    a, b, c = l0[...], l1[...], l2[...]
    m = jnp.maximum(jnp.maximum(a, b), c)
    ea, eb, ec = jnp.exp(a - m), jnp.exp(b - m), jnp.exp(c - m)
    den = ea + eb + ec
    out = o0[...] * (ea / den) + o1[...] * (eb / den) + o2[...] * (ec / den)
    out_ref[...] = out.astype(out_ref.dtype)


def dil_combine(os_, ls_):
    B, T, W = os_[0].shape
    tm = _tile(T, 512)
    spec = pl.BlockSpec((None, tm, W), lambda b, i: (b, i, 0))
    return pl.pallas_call(_combine_kernel, grid=(B, T // tm), in_specs=[spec] * 6, out_specs=spec,
                          out_shape=jax.ShapeDtypeStruct((B, T, W), BF16),
                          compiler_params=_cp("arbitrary", "arbitrary"))(*os_, *ls_)


def dil_prompt(proj_d, tab):
    B, T, _ = proj_d.shape
    nblk = SEG_D // D_W
    outs, lses = [], []
    for g, (w, d) in enumerate(DIL_PATTERNS):
        assert T % d == 0
        ts = T // d
        qb = min(Q_BLOCK, ts)
        nb = ts // qb
        bias = _band_bias(tab[:, g * DIL_SLOTS:(g + 1) * DIL_SLOTS], qb, qb, d, 0, DIL_NKEYS)
        view = proj_d.reshape(B, ts, d * SEG_D)
        qc, kc, vc = g, N_DIL + g, 2 * N_DIL + g

        def cur(col):
            return pl.BlockSpec((None, qb, D_W), lambda b, r, i, col=col: (b, i, r * nblk + col))

        def prev(col):
            return pl.BlockSpec((None, qb, D_W), lambda b, r, i, col=col: (b, jnp.maximum(i - 1, 0), r * nblk + col))

        ospec = pl.BlockSpec((None, qb, D_W), lambda b, r, i: (b, i, r))
        o, lse = banded_attention([view] * 5, [cur(qc), prev(kc), prev(vc), cur(kc), cur(vc)], bias, None,
                                  (B, d, nb), (B, ts, d * D_W), ospec,
                                  hk=DIL_SLOTS, rep=1, hd=DIL_HD, n_prev=qb, mask_first_prev=True,
                                  want_lse=True, blk_axis=2, out_dtype=F32)
        outs.append(o.reshape(B, T, D_W))
        lses.append(lse.reshape(B, T, D_W))
    return dil_combine(outs, lses)


def dil_sample(proj_d, caches, tab):
    B, T, _ = proj_d.shape
    outs, lses = [], []
    for g, (w, d) in enumerate(DIL_PATTERNS):
        cache = caches[g]
        P = cache.shape[2]
        bias = _strided_bias(tab[:, g * DIL_SLOTS:(g + 1) * DIL_SLOTS], T, P, d)
        qc, kc, vc = g, N_DIL + g, 2 * N_DIL + g
        specs = [pl.BlockSpec((None, T, D_W), lambda b, c=qc: (b, 0, c)),
                 pl.BlockSpec((None, None, P, D_W), lambda b: (b, 0, 0, 0)),
                 pl.BlockSpec((None, None, P, D_W), lambda b: (b, 1, 0, 0)),
                 pl.BlockSpec((None, T, D_W), lambda b, c=kc: (b, 0, c)),
                 pl.BlockSpec((None, T, D_W), lambda b, c=vc: (b, 0, c))]
        o, lse = banded_attention([proj_d, cache, cache, proj_d, proj_d], specs, bias, None, (B,), (B, T, D_W),
                                  pl.BlockSpec((None, T, D_W), lambda b: (b, 0, 0)),
                                  hk=DIL_SLOTS, rep=1, hd=DIL_HD, n_prev=P, mask_first_prev=False,
                                  want_lse=True, blk_axis=0, out_dtype=F32)
        outs.append(o)
        lses.append(lse)
    return dil_combine(outs, lses)


def _cumsum_rows(x, row):
    n = x.shape[0]
    k = 1
    while k < n:
        x = x + jnp.where(row >= k, pltpu.roll(x, k, 0), 0.0)
        k *= 2
    return x


def _hgrn_kernel(q_ref, f_ref, i_ref, g_ref, s0_ref, lga_ref, l1m_ref, oml_ref, nw_ref,
                 o_ref, sn_ref, st_scr, *, L, nchunks, has_s0):
    if has_s0:
        st_scr[...] = s0_ref[...].T
    else:
        st_scr[...] = jnp.zeros_like(st_scr)
    row = lax.broadcasted_iota(jnp.int32, (L, HG_DK), 0)
    lga, l1m, oml, nw = lga_ref[...], l1m_ref[...], oml_ref[...], nw_ref[...]

    def body(ci, carry):
        r0 = pl.multiple_of(ci * L, L)
        rows = pl.ds(r0, L)
        qr, f, v = q_ref[rows, :], f_ref[rows, :], i_ref[rows, :]
        qc = qr * jax.nn.sigmoid(qr)
        lsig = jnp.minimum(f, 0.0) - jnp.log1p(jnp.exp(-jnp.abs(f)))
        bb = l1m + lsig
        mx = jnp.maximum(lga, bb)
        lf = mx + jnp.log1p(jnp.exp(-jnp.abs(lga - bb)))
        k = oml * jax.nn.sigmoid(-f)
        b = _cumsum_rows(lf, row)
        st = st_scr[...]
        o = _dot_nt((qc * jnp.exp(b)).astype(BF16), st.astype(BF16))
        for s in range(L):
            dec = jnp.exp(jnp.where(row >= s, b - b[s:s + 1, :], -jnp.inf))
            a_s = jnp.sum(qc * k[s:s + 1, :] * dec, axis=-1, keepdims=True)
            o = o + a_s * v[s:s + 1, :]
        b_end = b[L - 1:L, :]
        ke = k * jnp.exp(b_end - b)
        st_scr[...] = st * jnp.exp(b_end) + _dot_tn(v.astype(BF16), ke.astype(BF16))
        gr = g_ref[rows, :]
        on = o * lax.rsqrt(jnp.mean(o * o, axis=-1, keepdims=True) + EPS) * nw * (gr * jax.nn.sigmoid(gr))
        o_ref[rows, :] = on.astype(o_ref.dtype)
        return carry

    lax.fori_loop(0, nchunks, body, 0)
    sn_ref[...] = st_scr[...].T


def hgrn_mixer(proj_h, s0, lb, norm_w):
    B, T, _ = proj_h.shape
    L = math.gcd(T, HG_CHUNK)
    H = HG_HEADS
    has_s0 = s0 is not None
    if s0 is None:
        s0 = jnp.zeros((1, 1, HG_DK, HG_DV), F32)
        s0_spec = pl.BlockSpec((None, None, HG_DK, HG_DV), lambda b, h: (0, 0, 0, 0))
    else:
        s0_spec = pl.BlockSpec((None, None, HG_DK, HG_DV), lambda b, h: (b, h, 0, 0))

    def col(seg):
        return pl.BlockSpec((None, T, HG_DK), lambda b, h, seg=seg: (b, 0, seg * H + h))

    def hv():
        return pl.BlockSpec((None, 1, HG_DK), lambda b, h: (h, 0, 0))

    lga = jnp.log(lb).reshape(H, 1, HG_DK)
    l1m = jnp.log1p(-lb).reshape(H, 1, HG_DK)
    oml = (1.0 - lb).reshape(H, 1, HG_DK)
    nw = norm_w.astype(F32).reshape(H, 1, HG_DV)
    o, sn = pl.pallas_call(
        functools.partial(_hgrn_kernel, L=L, nchunks=T // L, has_s0=has_s0),
        grid=(B, H),
        in_specs=[col(0), col(1), col(2), col(3), s0_spec, hv(), hv(), hv(), hv()],
        out_specs=[pl.BlockSpec((None, T, HG_DV), lambda b, h: (b, 0, h)),
                   pl.BlockSpec((None, None, HG_DK, HG_DV), lambda b, h: (b, h, 0, 0))],
        out_shape=[jax.ShapeDtypeStruct((B, T, H_W), BF16),
                   jax.ShapeDtypeStruct((B, H, HG_DK, HG_DV), F32)],
        scratch_shapes=[pltpu.VMEM((HG_DV, HG_DK), F32)],
        compiler_params=_cp("arbitrary", "arbitrary"),
    )(proj_h, proj_h, proj_h, proj_h, s0, lga, l1m, oml, nw)
    return o, sn


def _s5_kernel(u_ref, wb_ref, wc_ref, pre_ref, pim_ref, d_ref, wg_ref, h0r_ref, h0i_ref,
               o_ref, snr_ref, sni_ref, bu_scr, y_scr, cr_scr, ci_scr, *, rows_t, per_group, tiles_per_seq):
    ti = pl.program_id(1)
    ngrp = rows_t // 8
    row = lax.broadcasted_iota(jnp.int32, (8, SSM_BLK_ST), 0)
    if not per_group:
        @pl.when(ti % tiles_per_seq == 0)
        def _():
            cr_scr[...] = h0r_ref[...]
            ci_scr[...] = h0i_ref[...]

    for kb in range(SSM_NBLK):
        lanes = slice(kb * SSM_BLK_ST, (kb + 1) * SSM_BLK_ST)
        ucols = slice(kb * SSM_BLK_IN, (kb + 1) * SSM_BLK_IN)
        u_k = u_ref[:, ucols]
        bu_scr[...] = _dot(u_k.astype(BF16), wb_ref[kb].astype(BF16))
        pre, pim = pre_ref[:, lanes], pim_ref[:, lanes]

        def grp(gi, carry):
            cr, ci = carry
            r0 = pl.multiple_of(gi * 8, 8)
            xr = bu_scr[pl.ds(r0, 8), :SSM_BLK_ST]
            xi = bu_scr[pl.ds(r0, 8), SSM_BLK_ST:]
            for sh, arow in ((1, 0), (2, 1), (4, 3)):
                ar, ai = pre[arow:arow + 1, :], pim[arow:arow + 1, :]
                sr = jnp.where(row >= sh, pltpu.roll(xr, sh, 0), 0.0)
                si = jnp.where(row >= sh, pltpu.roll(xi, sh, 0), 0.0)
                xr, xi = xr + (ar * sr - ai * si), xi + (ar * si + ai * sr)
            if per_group:
                cr = h0r_ref[pl.ds(gi, 1), lanes]
                ci = h0i_ref[pl.ds(gi, 1), lanes]
            hr = xr + (pre * cr - pim * ci)
            hi = xi + (pre * ci + pim * cr)
            bu_scr[pl.ds(r0, 8), :SSM_BLK_ST] = hr
            bu_scr[pl.ds(r0, 8), SSM_BLK_ST:] = hi
            cr, ci = hr[7:8, :], hi[7:8, :]
            if per_group:
                snr_ref[pl.ds(gi, 1), lanes] = cr
                sni_ref[pl.ds(gi, 1), lanes] = ci
            return cr, ci

        if per_group:
            init = (jnp.zeros((1, SSM_BLK_ST), F32), jnp.zeros((1, SSM_BLK_ST), F32))
        else:
            init = (cr_scr[:, lanes], ci_scr[:, lanes])
        cr, ci = lax.fori_loop(0, ngrp, grp, init)
        if not per_group:
            cr_scr[:, lanes] = cr
            ci_scr[:, lanes] = ci
            snr_ref[:, lanes] = cr
            sni_ref[:, lanes] = ci
        y = _dot(bu_scr[...].astype(BF16), wc_ref[kb].astype(BF16)) + d_ref[:, ucols] * u_k
        y_scr[:, ucols] = jax.nn.gelu(y)
    yg = y_scr[...]
    z = yg * jax.nn.sigmoid(_dot(yg.astype(BF16), wg_ref[...].astype(BF16)))
    o_ref[...] = z.astype(o_ref.dtype)


def _ssm_params(a_re, a_im, b_re, b_im, c_re, c_im, log_dt):
    dt = jnp.exp(log_dt)[:, None]
    ar, ai = dt * a_re, dt * a_im
    mag = jnp.exp(ar)
    abar_re, abar_im = mag * jnp.cos(ai), mag * jnp.sin(ai)
    den = a_re * a_re + a_im * a_im
    nr, ni = abar_re - 1.0, abar_im
    coef_re = (nr * a_re + ni * a_im) / den
    coef_im = (ni * a_re - nr * a_im) / den
    bbar_re = coef_re[..., None] * b_re - coef_im[..., None] * b_im
    bbar_im = coef_re[..., None] * b_im + coef_im[..., None] * b_re
    n = jnp.arange(1, 9, dtype=F32)[:, None, None]
    pm = jnp.exp(n * ar)
    pre = (pm * jnp.cos(n * ai)).reshape(8, SSM_LANES)
    pim = (pm * jnp.sin(n * ai)).reshape(8, SSM_LANES)
    gpb = SSM_BLK_IN // SSM_GSIZE
    eye = jnp.eye(gpb, dtype=F32)

    def in_tiles(bb):
        t = bb.reshape(SSM_NBLK, gpb, SSM_P, SSM_GSIZE)
        t = jnp.einsum('kgpi,gh->kgihp', t, eye)
        return t.reshape(SSM_NBLK, SSM_BLK_IN, SSM_BLK_ST)

    def out_tiles(cc):
        t = cc.reshape(SSM_NBLK, gpb, SSM_GSIZE, SSM_P)
        t = jnp.einsum('kgip,gh->kgphi', t, eye)
        return t.reshape(SSM_NBLK, SSM_BLK_ST, SSM_BLK_IN)

    wb = jnp.concatenate([in_tiles(bbar_re), in_tiles(bbar_im)], axis=2)
    wc = jnp.concatenate([out_tiles(c_re), -out_tiles(c_im)], axis=1)
    return wb, wc, pre, pim


def s5_mixer(u, ssm, h0):
    wb, wc, pre, pim, d_skip, w_glu = ssm
    B, T, _ = u.shape
    per_group = T == 8
    if per_group:
        assert h0 is not None
        rows_t = B * T
        u2 = u.reshape(1, rows_t, SSM_CH)
        h0r, h0i = h0[:, 0], h0[:, 1]
        grid = (1, 1)
        hspec = pl.BlockSpec((B, SSM_LANES), lambda b, t: (0, 0))
        sspec = hspec
        s_shape = jax.ShapeDtypeStruct((B, SSM_LANES), F32)
        tiles_per_seq = 1
    else:
        rows_t = _tile(T, 256)
        u2 = u
        tiles_per_seq = T // rows_t
        grid = (B, tiles_per_seq)
        if h0 is None:
            h0 = jnp.zeros((B, 2, SSM_LANES), F32)
        h0r, h0i = h0[:, 0:1], h0[:, 1:2]
        hspec = pl.BlockSpec((None, 1, SSM_LANES), lambda b, t: (b, 0, 0))
        sspec = hspec
        s_shape = jax.ShapeDtypeStruct((B, 1, SSM_LANES), F32)
    const2 = lambda shape: pl.BlockSpec(shape, lambda b, t: (0,) * len(shape))
    out, snr, sni = pl.pallas_call(
        functools.partial(_s5_kernel, rows_t=rows_t, per_group=per_group, tiles_per_seq=tiles_per_seq),
        grid=grid,
        in_specs=[pl.BlockSpec((None, rows_t, SSM_CH), lambda b, t: (b, t, 0)),
                  const2(wb.shape), const2(wc.shape), const2(pre.shape), const2(pim.shape),
                  const2((1, SSM_CH)), const2(w_glu.shape), hspec, hspec],
        out_specs=[pl.BlockSpec((None, rows_t, SSM_CH), lambda b, t: (b, t, 0)), sspec, sspec],
        out_shape=[jax.ShapeDtypeStruct(u2.shape, BF16), s_shape, s_shape],
        scratch_shapes=[pltpu.VMEM((rows_t, 2 * SSM_BLK_ST), F32), pltpu.VMEM((rows_t, SSM_CH), F32),
                        pltpu.VMEM((1, SSM_LANES), F32), pltpu.VMEM((1, SSM_LANES), F32)],
        compiler_params=_cp("arbitrary", "arbitrary"),
    )(u2, wb, wc, pre, pim, d_skip.reshape(1, SSM_CH), w_glu, h0r, h0i)
    out = out.reshape(B, T, SSM_CH)
    state = jnp.stack([snr.reshape(B, SSM_LANES), sni.reshape(B, SSM_LANES)], axis=1)
    return out, state


def _router_kernel(x_ref, w_ref, b_ref, o_ref):
    logits = _dot(x_ref[...], w_ref[...].astype(BF16))
    s = jax.nn.sigmoid(logits)
    sb = s + b_ref[...]
    epg = N_EXPERTS // N_EXPERT_GROUPS
    sc = [s[:, j:j + 1] for j in range(N_EXPERTS)]
    bc = [sb[:, j:j + 1] for j in range(N_EXPERTS)]
    gscore = []
    for g in range(N_EXPERT_GROUPS):
        m = bc[g * epg:(g + 1) * epg]
        best = None
        for i in range(epg):
            for j in range(i + 1, epg):
                pair = m[i] + m[j]
                best = pair if best is None else jnp.maximum(best, pair)
        gscore.append(best)
    gsel, taken = [], None
    for g in range(N_EXPERT_GROUPS):
        ok = None
        for h in range(g + 1, N_EXPERT_GROUPS):
            c = gscore[g] >= gscore[h]
            ok = c if ok is None else ok & c
        if ok is None:
            ok = jnp.ones_like(gscore[g], dtype=jnp.bool_)
        if taken is not None:
            ok = ok & ~taken
        gsel.append(ok)
        taken = ok if taken is None else taken | ok
    chosen = []
    for g in range(N_EXPERT_GROUPS):
        for j in range(epg):
            e = g * epg + j
            rank = jnp.zeros_like(bc[e], dtype=jnp.int32)
            for k in range(epg):
                if k == j:
                    continue
                o = g * epg + k
                beats = (bc[o] > bc[e]) | ((bc[o] == bc[e]) & (k < j))
                rank = rank + beats.astype(jnp.int32)
            chosen.append(gsel[g] & (rank < TOP_K))
    den = None
    for e in range(N_EXPERTS):
        t = jnp.where(chosen[e], sc[e], 0.0)
        den = t if den is None else den + t
    for e in range(N_EXPERTS):
        comb = jnp.where(chosen[e], sc[e] / den, 0.0)
        o_ref[e] = jnp.broadcast_to(comb, o_ref.shape[1:])


def moe_router(xn3, router_w, router_b):
    G, R, D = xn3.shape
    tm = _tile(R, 256)
    return pl.pallas_call(
        _router_kernel, grid=(G, R // tm),
        in_specs=[pl.BlockSpec((None, tm, D), lambda b, i: (b, i, 0)),
                  pl.BlockSpec((D, N_EXPERTS), lambda b, i: (0, 0)),
                  pl.BlockSpec((1, N_EXPERTS), lambda b, i: (0, 0))],
        out_specs=pl.BlockSpec((None, N_EXPERTS, tm, 128), lambda b, i: (b, 0, i, 0)),
        out_shape=jax.ShapeDtypeStruct((G, N_EXPERTS, R, 128), F32),
        compiler_params=_cp("arbitrary", "arbitrary"),
    )(xn3, router_w, router_b.reshape(1, N_EXPERTS))


def _moe_kernel(x_ref, w1_ref, w3_ref, w2_ref, comb_ref, o_ref, *, tf):
    first = (pl.program_id(2) == 0) & (pl.program_id(3) == 0)
    x = x_ref[...]
    h1 = _dot(x, w1_ref[...].astype(BF16))
    h3 = _dot(x, w3_ref[...].astype(BF16))
    h = (h1 * jax.nn.sigmoid(h1)) * h3
    comb = comb_ref[...]
    h = jnp.concatenate([h[:, c * 128:(c + 1) * 128] * comb for c in range(tf // 128)], axis=1)
    y = _dot(h.astype(BF16), w2_ref[...].astype(BF16))

    @pl.when(first)
    def _():
        o_ref[...] = y

    @pl.when(jnp.logical_not(first))
    def _():
        o_ref[...] += y


def moe_dense(xn3, comb, w1, w3, w2, layer):
    G, R, D = xn3.shape
    F = w1.shape[-1]
    tm, tf = _tile(R, 512), _tile(F, 128, 128)
    return pl.pallas_call(
        functools.partial(_moe_kernel, tf=tf),
        grid=(G, R // tm, N_EXPERTS, F // tf),
        in_specs=[pl.BlockSpec((None, tm, D), lambda b, i, e, f: (b, i, 0)),
                  pl.BlockSpec((None, None, D, tf), lambda b, i, e, f: (layer, e, 0, f)),
                  pl.BlockSpec((None, None, D, tf), lambda b, i, e, f: (layer, e, 0, f)),
                  pl.BlockSpec((None, None, tf, D), lambda b, i, e, f: (layer, e, f, 0)),
                  pl.BlockSpec((None, None, tm, 128), lambda b, i, e, f: (b, e, i, 0))],
        out_specs=pl.BlockSpec((None, tm, D), lambda b, i, e, f: (b, i, 0)),
        out_shape=jax.ShapeDtypeStruct((G, R, D), F32),
        compiler_params=_cp("arbitrary", "arbitrary", "arbitrary", "arbitrary"),
    )(xn3, w1, w3, w2, comb)


def _trunk(x, mods, past, p, ssm_params, lbs):
    B, T, D = x.shape
    depth = p['w_in'].shape[0]
    prompt = past is None
    G, R = (B, T) if prompt else (1, B * T)
    x3 = x.reshape(G, R, D)
    tab = p['rel_bias']
    tab_a, tab_d = tab[:, :SWA_HQ], tab[:, SWA_HQ:]
    news = []
    _, xn = norm_stage(x3, p['norm_mix'][0], mods[0], (1, 0))
    y = None
    for l in range(depth):
        mod = mods[l]
        w_in = p['w_in']
        pa = matmul_seg(xn, w_in, l, OFF_A, SEG_A, 512, F32, tm_pref=1024).reshape(B, T, SEG_A)
        pd = matmul_seg(xn, w_in, l, OFF_D, SEG_D, 384, F32, tm_pref=1024).reshape(B, T, SEG_D)
        ph = matmul_seg(xn, w_in, l, OFF_H, SEG_H, 256, F32).reshape(B, T, SEG_H)
        ps = matmul_seg(xn, w_in, l, OFF_S, SEG_S, 256, F32).reshape(B, T, SEG_S)
        gates = matmul_seg(xn, w_in, l, OFF_G, 4 * D, 256, BF16, act="sigmoid")
        ka, va = pa[:, :, A_W:A_W + KV_W], pa[:, :, A_W + KV_W:]
        kd = [pd[:, :, (N_DIL + g) * D_W:(N_DIL + g + 1) * D_W] for g in range(N_DIL)]
        vd = [pd[:, :, (2 * N_DIL + g) * D_W:(2 * N_DIL + g + 1) * D_W] for g in range(N_DIL)]
        if prompt:
            oa = swa_prompt(pa, p['swa_sinks'][l], tab_a)
            od = dil_prompt(pd, tab_d)
            oh, new_h = hgrn_mixer(ph, None, lbs[l], p['hgrn_norm'][l])
            osm, new_s = s5_mixer(ps, ssm_params[l], None)
            nk = min(SWA_WINDOW, T)
            new_swa = jnp.stack([ka[:, -nk:], va[:, -nk:]], axis=1)
            new_dil = [jnp.stack([kd[g][:, -min(w, T):], vd[g][:, -min(w, T):]], axis=1)
                       for g, (w, _) in enumerate(DIL_PATTERNS)]
        else:
            c_swa, c_dil, c_h, c_s = past
            cs = c_swa[l].reshape(B, 2, -1, KV_W)
            cd = [c[l].reshape(B, 2, -1, D_W) for c in c_dil]
            oa = swa_sample(pa, cs, p['swa_sinks'][l], tab_a)
            od = dil_sample(pd, cd, tab_d)
            oh, new_h = hgrn_mixer(ph, c_h[l], lbs[l], p['hgrn_norm'][l])
            osm, new_s = s5_mixer(ps, ssm_params[l], c_s[l].reshape(B, 2, SSM_LANES))
            nk = min(SWA_WINDOW, cs.shape[2] + T)
            new_swa = jnp.stack([jnp.concatenate([cs[:, 0], ka], axis=1)[:, -nk:],
                                 jnp.concatenate([cs[:, 1], va], axis=1)[:, -nk:]], axis=1)
            new_dil = []
            for g, (w, _) in enumerate(DIL_PATTERNS):
                n_keep = min(w, cd[g].shape[2] + T)
                new_dil.append(jnp.stack([jnp.concatenate([cd[g][:, 0], kd[g]], axis=1)[:, -n_keep:],
                                          jnp.concatenate([cd[g][:, 1], vd[g]], axis=1)[:, -n_keep:]], axis=1))
        news.append((new_swa.reshape(B, 2, -1, SWA_HKV, SWA_HD),
                     new_dil[0].reshape(B, 2, -1, DIL_SLOTS, DIL_HD),
                     new_dil[1].reshape(B, 2, -1, DIL_SLOTS, DIL_HD),
                     new_dil[2].reshape(B, 2, -1, DIL_SLOTS, DIL_HD),
                     new_h, new_s.reshape(B, 2, SSM_GROUPS, SSM_P)))
        merged = merge_branches(oa.reshape(G, R, A_W), oh.reshape(G, R, H_W), osm.reshape(G, R, SSM_CH),
                                od.reshape(G, R, D_W), gates, p['w_br_a'], p['w_br_h'], p['w_br_s'],
                                p['w_br_d'], l)
        mix = matmul_seg(merged, p['w_out'], l, 0, D, 256, F32)
        x3, xn2 = norm_stage(x3, p['norm_ffn'][l], mod, (4, 3), resid=mix, gate_chunk=2)
        comb = moe_router(xn2, p['router_w'], p['router_b'])
        ffn = moe_dense(xn2, comb, p['moe_w1'], p['moe_w3'], p['moe_w2'], l)
        if l + 1 < depth:
            x3, xn = norm_stage(x3, p['norm_mix'][l + 1], mods[l + 1], (1, 0), resid=ffn, gate_mod=mod,
                                gate_chunk=5)
        else:
            mod_none = mod
            _, y = _final_norm(x3, ffn, mod_none, p['final_norm'])
    stacked = [jnp.stack([nw[i] for nw in news], axis=0) for i in range(6)]
    return y.reshape(B, T, D), stacked


def _final_norm(x3, ffn, mod, g):
    G, R, D = x3.shape
    tm = _tile(R, 256)
    rm = mod.shape[1]
    xspec = pl.BlockSpec((None, tm, D), lambda b, i: (b, i, 0))
    if rm == 1:
        gspec = pl.BlockSpec((None, 1, D), lambda b, i: (b, 0, 5))
    else:
        gspec = pl.BlockSpec((None, tm, D), lambda b, i: (b, i, 5))
    outs = pl.pallas_call(
        functools.partial(_norm_kernel, has_resid=True, has_mod=False, emit_x=False),
        grid=(G, R // tm),
        in_specs=[xspec, xspec, gspec, pl.BlockSpec((1, D), lambda b, i: (0, 0))],
        out_specs=[xspec], out_shape=[jax.ShapeDtypeStruct((G, R, D), F32)],
        compiler_params=_cp("arbitrary", "arbitrary"),
    )(x3, ffn, mod, g.reshape(1, D))
    return None, outs[0]


def kernel(x_prompt, x_sample, c_prompt, c_sample, cache_swa, cache_dil1, cache_dil2, cache_dil3, state_hgrn, state_ssm, w_ada, b_ada, norm_mix, norm_ffn, w_in, swa_sinks, rel_bias, hgrn_lb, hgrn_norm, ssm_a_re, ssm_a_im, ssm_b_re, ssm_b_im, ssm_c_re, ssm_c_im, ssm_d, ssm_log_dt, ssm_w_glu, w_br_a, w_br_h, w_br_s, w_br_d, w_out, router_w, router_b, moe_w1, moe_w3, moe_w2, final_norm):
    p = {'norm_mix': norm_mix, 'norm_ffn': norm_ffn, 'w_in': w_in, 'swa_sinks': swa_sinks, 'rel_bias': rel_bias,
         'hgrn_norm': hgrn_norm, 'w_br_a': w_br_a, 'w_br_h': w_br_h, 'w_br_s': w_br_s, 'w_br_d': w_br_d,
         'w_out': w_out, 'router_w': router_w, 'router_b': router_b, 'moe_w1': moe_w1, 'moe_w3': moe_w3,
         'moe_w2': moe_w2, 'final_norm': final_norm}
    depth = w_in.shape[0]
    bp, bs = x_prompt.shape[0], x_sample.shape[0]
    ts = x_sample.shape[1]
    pad = (-(bp + bs)) % 8
    c_all = jnp.concatenate([c_prompt, c_sample, jnp.zeros((pad, c_prompt.shape[1]), F32)], axis=0)
    mod_all = ada_modulation(c_all, w_ada, b_ada)
    mods_p = [mod_all[l, :bp][:, None, :] for l in range(depth)]
    mods_s = [jnp.repeat(mod_all[l, bp:bp + bs], ts, axis=0)[None] for l in range(depth)]
    lb_all = jnp.cumsum(jax.nn.softmax(hgrn_lb.astype(F32), axis=0), axis=0)
    lbs = [(lb_all[l] - lb_all[0]).reshape(HG_HEADS, HG_DK) for l in range(depth)]
    ssm_params = [(*_ssm_params(ssm_a_re[l], ssm_a_im[l], ssm_b_re[l], ssm_b_im[l], ssm_c_re[l], ssm_c_im[l],
                                ssm_log_dt[l]), ssm_d[l], ssm_w_glu[l]) for l in range(depth)]
    y_p, st_p = _trunk(x_prompt, mods_p, None, p, ssm_params, lbs)
    past = (cache_swa, (cache_dil1, cache_dil2, cache_dil3), state_hgrn, state_ssm)
    y_s, st_s = _trunk(x_sample, mods_s, past, p, ssm_params, lbs)
    return (y_p, y_s, st_p[0], st_p[1], st_p[2], st_p[3], st_p[4], st_p[5],
            st_s[0], st_s[1], st_s[2], st_s[3], st_s[4], st_s[5])
```
